```python
import math
import jax, jax.numpy as jnp
from jax import lax
import numpy as np

D_MODEL = 2048
BATCH = 8
SEQ = 2048
DEPTH = 1
DEC_BATCH = 128
DEC_SEQ = 8
PAST_LEN = 2048
PAGE_SIZE = 128

GLA_HEADS = 4
GLA_DK = 128
GLA_DV = 256
GLA_KW = GLA_HEADS * GLA_DK
GLA_VW = GLA_HEADS * GLA_DV
GLA_RANK = 16
GLA_TAU = 16.0
GLA_CHUNK = 64
FOX_HEADS = 8
FOX_DH = 128
FOX_W = FOX_HEADS * FOX_DH
FOX_BLOCK = 128
D_FF = ((8 * D_MODEL + 3 * 256 - 1) // (3 * 256)) * 256
EPS = 1e-6
IN_SPLITS = (GLA_KW, GLA_KW, GLA_VW, GLA_VW, GLA_RANK, FOX_W, FOX_W, FOX_W, FOX_HEADS, D_MODEL, D_MODEL)
D_IN = 2 * GLA_KW + 2 * GLA_VW + GLA_RANK + 3 * FOX_W + FOX_HEADS + 2 * D_MODEL

kernel_name = 'gla_fox_parallel_hybrid_step'


def rmsnorm(x, g):
    xf = x.astype(jnp.float32)
    y = xf * lax.rsqrt(jnp.mean(xf * xf, axis=-1, keepdims=True) + EPS)
    return (y * g.astype(jnp.float32)).astype(x.dtype)


def branch_inputs(x, g_attn, w_in, w_gla_a2, b_gla_a, b_fox_f):
    B, L = x.shape[0], x.shape[1]
    f32 = jnp.float32
    z = rmsnorm(x, g_attn) @ w_in
    offs = [int(o) for o in np.cumsum(IN_SPLITS)[:-1]]
    gq, gk, gv, gr, ga1, fq, fk, fv, ff, gate_a, gate_b = jnp.split(z, offs, axis=-1)
    heads = lambda t, h, d: t.astype(f32).reshape(B, L, h, d)
    log_a = jax.nn.log_sigmoid((ga1 @ w_gla_a2 + b_gla_a).astype(f32)) / GLA_TAU
    lf = jax.nn.log_sigmoid((ff + b_fox_f).astype(f32))
    return (heads(gq, GLA_HEADS, GLA_DK) * GLA_DK ** -0.5, heads(gk, GLA_HEADS, GLA_DK),
            heads(gv, GLA_HEADS, GLA_DV), gr, heads(log_a, GLA_HEADS, GLA_DK),
            heads(fq, FOX_HEADS, FOX_DH), heads(fk, FOX_HEADS, FOX_DH), heads(fv, FOX_HEADS, FOX_DH),
            lf, gate_a, gate_b)


def gla_chunked(q, k, v, log_a, s0):
    B, L = q.shape[0], q.shape[1]
    C = math.gcd(L, GLA_CHUNK)
    n = L // C
    to_chunks = lambda t: t.reshape(B, n, C, t.shape[2], t.shape[3]).transpose(1, 0, 3, 2, 4)
    causal = jnp.tril(jnp.ones((C, C), dtype=bool))

    def step(S, inp):
        qi, ki, vi, ai = inp
        b = jnp.cumsum(ai, axis=-2)
        q_dec = qi * jnp.exp(b)
        k_inv = ki * jnp.exp(-b)
        attn = jnp.where(causal, jnp.einsum('bhtd,bhsd->bhts', q_dec, k_inv), 0.0)
        o = jnp.einsum('bhtd,bhdv->bhtv', q_dec, S) + jnp.einsum('bhts,bhsv->bhtv', attn, vi)
        k_end = ki * jnp.exp(b[..., -1:, :] - b)
        S = S * jnp.exp(b[..., -1, :])[..., None] + jnp.einsum('bhsd,bhsv->bhdv', k_end, vi)
        return S, o

    S_fin, o = lax.scan(step, s0.astype(jnp.float32), (to_chunks(q), to_chunks(k), to_chunks(v), to_chunks(log_a)))
    o = o.transpose(1, 0, 3, 2, 4).reshape(B, L, q.shape[2], v.shape[3])
    return o, S_fin


def fox_prompt(q, k, v, lf):
    B, S, H, Dh = q.shape
    nb = S // FOX_BLOCK
    c = jnp.cumsum(lf, axis=1)
    cT = c.transpose(0, 2, 1)
    pos = jnp.arange(S)
    qb = q.reshape(B, nb, FOX_BLOCK, H, Dh).transpose(1, 0, 2, 3, 4)
    cb = cT.reshape(B, H, nb, FOX_BLOCK).transpose(2, 0, 1, 3)
    pb = pos.reshape(nb, FOX_BLOCK)
    scale = FOX_DH ** -0.5

    def block(args):
        qi, ci, pi = args
        s = jnp.einsum('bqhd,bkhd->bhqk', qi, k) * scale + (ci[..., None] - cT[:, :, None, :])
        s = jnp.where(pi[:, None] >= pos[None, :], s, -jnp.inf)
        p = jax.nn.softmax(s, axis=-1)
        return jnp.einsum('bhqk,bkhd->bqhd', p, v)

    o = lax.map(block, (qb, cb, pb))
    return o.transpose(1, 0, 2, 3, 4).reshape(B, S, H, Dh)


def fox_sample(q, k, v, lf, k_cache, v_cache, lf_cache, page_table):
    B, T, H, Dh = q.shape
    n_pages = page_table.shape[1]
    scale = FOX_DH ** -0.5
    lf_past = lf_cache[page_table].astype(jnp.float32).reshape(B, n_pages * PAGE_SIZE, H)
    c_past = jnp.cumsum(lf_past, axis=1)
    c_past = c_past - c_past[:, -1:, :]
    cq = jnp.cumsum(lf, axis=1).transpose(0, 2, 1)
    c_pages = c_past.reshape(B, n_pages, PAGE_SIZE, H).transpose(1, 0, 3, 2)

    def page_step(carry, inp):
        m, l, acc = carry
        pt, cp = inp
        kp = k_cache[pt].astype(jnp.float32)
        vp = v_cache[pt].astype(jnp.float32)
        s = jnp.einsum('bthd,bphd->bhtp', q, kp) * scale + (cq[..., None] - cp[:, :, None, :])
        m_new = jnp.maximum(m, s.max(axis=-1))
        corr = jnp.exp(m - m_new)
        p = jnp.exp(s - m_new[..., None])
        return (m_new, l * corr + p.sum(axis=-1), acc * corr[..., None] + jnp.einsum('bhtp,bphd->bhtd', p, vp)), None

    init = (jnp.full((B, H, T), -jnp.inf, jnp.float32), jnp.zeros((B, H, T), jnp.float32),
            jnp.zeros((B, H, T, Dh), jnp.float32))
    (m, l, acc), _ = lax.scan(page_step, init, (page_table.T, c_pages))
    s = jnp.einsum('bthd,bshd->bhts', q, k) * scale + (cq[..., None] - cq[:, :, None, :])
    s = jnp.where(jnp.tril(jnp.ones((T, T), dtype=bool)), s, -jnp.inf)
    m_new = jnp.maximum(m, s.max(axis=-1))
    corr = jnp.exp(m - m_new)
    p = jnp.exp(s - m_new[..., None])
    l = l * corr + p.sum(axis=-1)
    acc = acc * corr[..., None] + jnp.einsum('bhts,bshd->bhtd', p, v)
    return (acc / l[..., None]).transpose(0, 2, 1, 3)


def layer_tail(x, o_gla, gr, o_fox, gate_a, gate_b, g_gla_norm, w_pa, w_pb, w_o, g_ffn, w_gate, w_up, w_down):
    B, L = x.shape[0], x.shape[1]
    o_gla = o_gla * lax.rsqrt(jnp.mean(o_gla * o_gla, axis=-1, keepdims=True) + EPS)
    o_gla = (o_gla.reshape(B, L, GLA_VW) * g_gla_norm.astype(jnp.float32)).astype(x.dtype) * jax.nn.silu(gr)
    o_fox = o_fox.reshape(B, L, FOX_W).astype(x.dtype)
    merged = jax.nn.sigmoid(gate_a) * (o_gla @ w_pa) + jax.nn.sigmoid(gate_b) * (o_fox @ w_pb)
    x = x + merged @ w_o
    h = rmsnorm(x, g_ffn)
    return x + (jax.nn.silu(h @ w_gate) * (h @ w_up)) @ w_down


def setup_inputs(seed: int = 0) -> dict:
    key = jax.random.key(seed)
    ks = jax.random.split(key, 24)
    f32 = jnp.float32
    n_pages = PAST_LEN // PAGE_SIZE
    n_pool = (DEC_BATCH * n_pages * 5) // 4
    nrm = lambda k, shape, scale: jax.random.normal(k, shape, f32) * scale
    gain = lambda k, shape: 1.0 + 0.02 * jax.random.normal(k, shape, f32)
    page_table = jax.random.permutation(ks[6], n_pool)[: DEC_BATCH * n_pages].reshape(DEC_BATCH, n_pages).astype(jnp.int32)
    return {
        'x_prompt': nrm(ks[0], (BATCH, SEQ, D_MODEL), 1.0),
        'x_sample': nrm(ks[1], (DEC_BATCH, DEC_SEQ, D_MODEL), 1.0),
        'cache_k': nrm(ks[2], (DEPTH, n_pool, PAGE_SIZE, FOX_HEADS, FOX_DH), 1.0),
        'cache_v': nrm(ks[3], (DEPTH, n_pool, PAGE_SIZE, FOX_HEADS, FOX_DH), 1.0),
        'cache_lf': jax.nn.log_sigmoid(2.0 + jax.random.normal(ks[4], (DEPTH, n_pool, PAGE_SIZE, FOX_HEADS), f32)),
        'state_gla': nrm(ks[5], (DEPTH, DEC_BATCH, GLA_HEADS, GLA_DK, GLA_DV), 0.5),
        'page_table': page_table,
        'g_attn': gain(ks[7], (DEPTH, D_MODEL)),
        'w_in': nrm(ks[8], (DEPTH, D_MODEL, D_IN), D_MODEL ** -0.5),
        'w_gla_a2': nrm(ks[9], (DEPTH, GLA_RANK, GLA_KW), GLA_RANK ** -0.5),
        'b_gla_a': nrm(ks[10], (DEPTH, GLA_KW), 0.1),
        'b_fox_f': 2.0 + nrm(ks[11], (DEPTH, FOX_HEADS), 0.1),
        'g_gla_norm': gain(ks[12], (DEPTH, GLA_VW)),
        'w_pa': nrm(ks[13], (DEPTH, GLA_VW, D_MODEL), GLA_VW ** -0.5),
        'w_pb': nrm(ks[14], (DEPTH, FOX_W, D_MODEL), FOX_W ** -0.5),
        'w_o': nrm(ks[15], (DEPTH, D_MODEL, D_MODEL), D_MODEL ** -0.5),
        'g_ffn': gain(ks[16], (DEPTH, D_MODEL)),
        'w_gate': nrm(ks[17], (DEPTH, D_MODEL, D_FF), D_MODEL ** -0.5),
        'w_up': nrm(ks[18], (DEPTH, D_MODEL, D_FF), D_MODEL ** -0.5),
        'w_down': nrm(ks[19], (DEPTH, D_FF, D_MODEL), D_FF ** -0.5),
        'g_final': gain(ks[20], (D_MODEL,)),
    }


def reference(x_prompt, x_sample, cache_k, cache_v, cache_lf, state_gla, page_table, g_attn, w_in, w_gla_a2,
              b_gla_a, b_fox_f, g_gla_norm, w_pa, w_pb, w_o, g_ffn, w_gate, w_up, w_down, g_final):
    yp, ys = x_prompt, x_sample
    s_p, s_s, k_p, v_p, lf_p, k_s, v_s, lf_s = ([] for _ in range(8))
    for l in range(DEPTH):
        tail_w = (g_gla_norm[l], w_pa[l], w_pb[l], w_o[l], g_ffn[l], w_gate[l], w_up[l], w_down[l])
        gq, gk, gv, gr, log_a, fq, fk, fv, lf, ga, gb = branch_inputs(yp, g_attn[l], w_in[l], w_gla_a2[l], b_gla_a[l], b_fox_f[l])
        s0 = jnp.zeros((yp.shape[0], GLA_HEADS, GLA_DK, GLA_DV), jnp.float32)
        o_a, s_fin = gla_chunked(gq, gk, gv, log_a, s0)
        o_b = fox_prompt(fq, fk, fv, lf)
        yp = layer_tail(yp, o_a, gr, o_b, ga, gb, *tail_w)
        s_p.append(s_fin)
        k_p.append(fk)
        v_p.append(fv)
        lf_p.append(lf)
        gq, gk, gv, gr, log_a, fq, fk, fv, lf, ga, gb = branch_inputs(ys, g_attn[l], w_in[l], w_gla_a2[l], b_gla_a[l], b_fox_f[l])
        o_a, s_fin = gla_chunked(gq, gk, gv, log_a, state_gla[l])
        o_b = fox_sample(fq, fk, fv, lf, cache_k[l], cache_v[l], cache_lf[l], page_table)
        ys = layer_tail(ys, o_a, gr, o_b, ga, gb, *tail_w)
        s_s.append(s_fin)
        k_s.append(fk)
        v_s.append(fv)
        lf_s.append(lf)
    y_prompt = rmsnorm(yp, g_final)
    y_sample = rmsnorm(ys, g_final)
    return (y_prompt, y_sample, jnp.stack(s_p), jnp.stack(s_s), jnp.stack(k_p), jnp.stack(v_p), jnp.stack(lf_p),
            jnp.stack(k_s), jnp.stack(v_s), jnp.stack(lf_s))
```

```python
import functools

import jax
import jax.numpy as jnp
from jax import lax
from jax.experimental import pallas as pl
from jax.experimental.pallas import tpu as pltpu

F32 = jnp.float32
BF16 = jnp.bfloat16
HIGHEST = lax.Precision.HIGHEST

EPS = 1e-6
GLA_HEADS = 4
GLA_DK = 128
GLA_DV = 256
GLA_KW = GLA_HEADS * GLA_DK
GLA_VW = GLA_HEADS * GLA_DV
GLA_RANK = 16
GLA_TAU = 16.0
GLA_CHUNK = 64
FOX_HEADS = 8
FOX_DH = 128
FOX_W = FOX_HEADS * FOX_DH
PAGE_SIZE = 128
SMALL_W = 128

OFF_GQ = 0
OFF_GK = OFF_GQ + GLA_KW
OFF_GV = OFF_GK + GLA_KW
OFF_GR = OFF_GV + GLA_VW
OFF_FQ = OFF_GR + GLA_VW
OFF_FK = OFF_FQ + FOX_W
OFF_FV = OFF_FK + FOX_W
OFF_GA = OFF_FV + FOX_W

VMEM_LIMIT = 52 * 1024 * 1024


def _params(sem, vmem=VMEM_LIMIT):
    return pltpu.CompilerParams(dimension_semantics=sem, vmem_limit_bytes=vmem)


def _log_sigmoid(x):
    return jnp.minimum(x, 0.0) - jnp.log1p(jnp.exp(-jnp.abs(x)))


def _sigmoid(x):
    return 1.0 / (1.0 + jnp.exp(-x))


def _silu(x):
    return x * _sigmoid(x)


def _rmsnorm(x, g):
    return x * lax.rsqrt(jnp.mean(x * x, axis=-1, keepdims=True) + EPS) * g


def _dot(a, b, **kw):
    return jnp.dot(a, b, preferred_element_type=F32, **kw)


def _dot_nt(a, b):
    return lax.dot_general(a, b, (((1,), (1,)), ((), ())), preferred_element_type=F32)


def _dot_tn(a, b):
    return lax.dot_general(a, b, (((0,), (0,)), ((), ())), preferred_element_type=F32)


def _in_proj_kernel(x_ref, g_ref, w_ref, ws_ref, bf_ref, z_ref, zs_ref, lf_ref, h_ref):
    @pl.when(pl.program_id(1) == 0)
    def _():
        hb = _rmsnorm(x_ref[...], g_ref[...]).astype(BF16)
        h_ref[...] = hb
        zs = _dot(hb, ws_ref[...])
        zs_ref[...] = zs
        lf_ref[...] = _log_sigmoid(zs[:, GLA_RANK:GLA_RANK + FOX_HEADS] + bf_ref[...])

    z_ref[...] = _dot(h_ref[...], w_ref[...])


def _in_proj(x, g, w_main, w_small, b_fox, *, tm, tn):
    t, d = x.shape
    n = w_main.shape[1]
    return pl.pallas_call(
        _in_proj_kernel,
        grid=(t // tm, n // tn),
        in_specs=[
            pl.BlockSpec((tm, d), lambda i, j: (i, 0)),
            pl.BlockSpec((1, d), lambda i, j: (0, 0)),
            pl.BlockSpec((d, tn), lambda i, j: (0, j)),
            pl.BlockSpec((d, SMALL_W), lambda i, j: (0, 0)),
            pl.BlockSpec((1, FOX_HEADS), lambda i, j: (0, 0)),
        ],
        out_specs=[
            pl.BlockSpec((tm, tn), lambda i, j: (i, j)),
            pl.BlockSpec((tm, SMALL_W), lambda i, j: (i, 0)),
            pl.BlockSpec((tm, FOX_HEADS), lambda i, j: (i, 0)),
        ],
        out_shape=[
            jax.ShapeDtypeStruct((t, n), F32),
            jax.ShapeDtypeStruct((t, SMALL_W), F32),
            jax.ShapeDtypeStruct((t, FOX_HEADS), F32),
        ],
        scratch_shapes=[pltpu.VMEM((tm, d), BF16)],
        compiler_params=_params(("parallel", "arbitrary")),
        name="in_proj",
    )(x, g, w_main, w_small, b_fox)


def _gla_kernel(*refs, chunk, has_init):
    if has_init:
        (q_ref, k_ref, v_ref, gr_ref, zs_ref, wa2_ref, ba_ref, gn_ref, s0_ref,
         o_ref, sfin_ref, s_scr) = refs
    else:
        (q_ref, k_ref, v_ref, gr_ref, zs_ref, wa2_ref, ba_ref, gn_ref,
         o_ref, sfin_ref, s_scr) = refs
    c = pl.program_id(1)

    @pl.when(c == 0)
    def _():
        if has_init:
            s_scr[...] = s0_ref[0]
        else:
            s_scr[...] = jnp.zeros_like(s_scr)

    ga1 = zs_ref[:, 0:GLA_RANK].astype(BF16)
    log_a = _log_sigmoid(_dot(ga1, wa2_ref[...]) + ba_ref[...]) / GLA_TAU
    row = lax.broadcasted_iota(jnp.int32, (chunk, chunk), 0)
    col = lax.broadcasted_iota(jnp.int32, (chunk, chunk), 1)
    causal = row >= col
    b = _dot(causal.astype(F32), log_a, precision=HIGHEST)
    b_last = b[chunk - 1:chunk, :]
    q = q_ref[...] * (GLA_DK ** -0.5)
    k = k_ref[...]
    q_dec = (q * jnp.exp(b)).astype(BF16)
    k_inv = (k * jnp.exp(-b)).astype(BF16)
    k_end = (k * jnp.exp(b_last - b)).astype(BF16)
    d_last = jnp.exp(b_last)

    for h in range(GLA_HEADS):
        ks = slice(h * GLA_DK, (h + 1) * GLA_DK)
        vs = slice(h * GLA_DV, (h + 1) * GLA_DV)
        attn = jnp.where(causal, _dot_nt(q_dec[:, ks], k_inv[:, ks]), 0.0)
        s_old = s_scr[h]
        vh = v_ref[:, vs].astype(BF16)
        o = _dot(q_dec[:, ks], s_old.astype(BF16)) + _dot(attn.astype(BF16), vh)
        d_col = jnp.transpose(jnp.broadcast_to(d_last[:, ks], (GLA_DK, GLA_DK)))
        s_scr[h] = s_old * jnp.concatenate([d_col, d_col], axis=1) + _dot_tn(k_end[:, ks], vh)
        o = o * lax.rsqrt(jnp.mean(o * o, axis=-1, keepdims=True) + EPS)
        o_ref[:, vs] = ((o * gn_ref[:, vs]) * _silu(gr_ref[:, vs])).astype(o_ref.dtype)

    @pl.when(c == pl.num_programs(1) - 1)
    def _():
        sfin_ref[0] = s_scr[...]


def _gla(z, zs, wa2, ba, gn, s0, *, batch, seq, out_dtype):
    chunk = GLA_CHUNK if seq % GLA_CHUNK == 0 else 8
    assert seq % chunk == 0
    n = seq // chunk
    t = batch * seq
    rows = lambda b, c: b * n + c
    kw, vw = GLA_KW, GLA_VW
    in_specs = [
        pl.BlockSpec((chunk, kw), lambda b, c: (rows(b, c), OFF_GQ // kw)),
        pl.BlockSpec((chunk, kw), lambda b, c: (rows(b, c), OFF_GK // kw)),
        pl.BlockSpec((chunk, vw), lambda b, c: (rows(b, c), OFF_GV // vw)),
        pl.BlockSpec((chunk, vw), lambda b, c: (rows(b, c), OFF_GR // vw)),
        pl.BlockSpec((chunk, SMALL_W), lambda b, c: (rows(b, c), 0)),
        pl.BlockSpec((GLA_RANK, kw), lambda b, c: (0, 0)),
        pl.BlockSpec((1, kw), lambda b, c: (0, 0)),
        pl.BlockSpec((1, vw), lambda b, c: (0, 0)),
    ]
    args = [z, z, z, z, zs, wa2, ba, gn]
    state_spec = pl.BlockSpec((1, GLA_HEADS, GLA_DK, GLA_DV), lambda b, c: (b, 0, 0, 0))
    if s0 is not None:
        in_specs.append(state_spec)
        args.append(s0)
    return pl.pallas_call(
        functools.partial(_gla_kernel, chunk=chunk, has_init=s0 is not None),
        grid=(batch, n),
        in_specs=in_specs,
        out_specs=[pl.BlockSpec((chunk, vw), lambda b, c: (rows(b, c), 0)), state_spec],
        out_shape=[
            jax.ShapeDtypeStruct((t, vw), out_dtype),
            jax.ShapeDtypeStruct((batch, GLA_HEADS, GLA_DK, GLA_DV), F32),
        ],
        scratch_shapes=[pltpu.VMEM((GLA_HEADS, GLA_DK, GLA_DV), F32)],
        compiler_params=_params(("parallel", "arbitrary")),
        name="gla",
    )(*args)


def _cumsum_kernel(lf_ref, c_ref, carry_ref, *, blk):
    @pl.when(pl.program_id(1) == 0)
    def _():
        carry_ref[...] = jnp.zeros_like(carry_ref)

    row = lax.broadcasted_iota(jnp.int32, (blk, blk), 0)
    col = lax.broadcasted_iota(jnp.int32, (blk, blk), 1)
    c = _dot((row >= col).astype(F32), lf_ref[0], precision=HIGHEST) + carry_ref[...]
    c_ref[0] = c
    carry_ref[...] = c[blk - 1:blk, :]


def _fox_cumsum(lf, *, blk):
    b, s, h = lf.shape
    return pl.pallas_call(
        functools.partial(_cumsum_kernel, blk=blk),
        grid=(b, s // blk),
        in_specs=[pl.BlockSpec((1, blk, h), lambda i, j: (i, j, 0))],
        out_specs=pl.BlockSpec((1, blk, h), lambda i, j: (i, j, 0)),
        out_shape=jax.ShapeDtypeStruct((b, s, h), F32),
        scratch_shapes=[pltpu.VMEM((1, h), F32)],
        compiler_params=_params(("parallel", "arbitrary")),
        name="fox_cumsum",
    )(lf)


def _fox_prompt_kernel(q_ref, k_ref, v_ref, cc_ref, cr_ref, o_ref, *, tq, tk):
    h = pl.program_id(1)
    qi = pl.program_id(2)
    scale = FOX_DH ** -0.5
    q = q_ref[...].astype(BF16)
    lane = lax.broadcasted_iota(jnp.int32, (tq, FOX_HEADS), 1)
    cq = jnp.sum(jnp.where(lane == h, cc_ref[0], 0.0), axis=1, keepdims=True)
    q_pos = qi * tq + lax.broadcasted_iota(jnp.int32, (tq, tk), 0)
    k_off = lax.broadcasted_iota(jnp.int32, (tq, tk), 1)

    def body(kb, carry):
        m, l, acc = carry
        start = pl.multiple_of(kb * tk, tk)
        kk = k_ref[pl.ds(start, tk), :].astype(BF16)
        vv = v_ref[pl.ds(start, tk), :].astype(BF16)
        ck = cr_ref[0, :, pl.ds(start, tk)]
        s = _dot_nt(q, kk) * scale + (cq - ck)
        s = jnp.where(q_pos >= k_off + start, s, -jnp.inf)
        m_new = jnp.maximum(m, jnp.max(s, axis=1, keepdims=True))
        corr = jnp.exp(m - m_new)
        p = jnp.exp(s - m_new)
        l = l * corr + jnp.sum(p, axis=1, keepdims=True)
        acc = acc * corr + _dot(p.astype(BF16), vv)
        return m_new, l, acc

    init = (jnp.full((tq, 1), -jnp.inf, F32), jnp.zeros((tq, 1), F32), jnp.zeros((tq, FOX_DH), F32))
    n_kb = (qi * tq + tq + tk - 1) // tk
    m, l, acc = lax.fori_loop(0, n_kb, body, init)
    o_ref[...] = (acc / l).astype(o_ref.dtype)


def _fox_prompt(z, c_col, c_row, *, batch, seq, tq, tk):
    nq = seq // tq
    dh = FOX_DH
    return pl.pallas_call(
        functools.partial(_fox_prompt_kernel, tq=tq, tk=tk),
        grid=(batch, FOX_HEADS, nq),
        in_specs=[
            pl.BlockSpec((tq, dh), lambda b, h, i: (b * nq + i, OFF_FQ // dh + h)),
            pl.BlockSpec((seq, dh), lambda b, h, i: (b, OFF_FK // dh + h)),
            pl.BlockSpec((seq, dh), lambda b, h, i: (b, OFF_FV // dh + h)),
            pl.BlockSpec((1, tq, FOX_HEADS), lambda b, h, i: (b, i, 0)),
            pl.BlockSpec((1, 1, seq), lambda b, h, i: (b * FOX_HEADS + h, 0, 0)),
        ],
        out_specs=pl.BlockSpec((tq, dh), lambda b, h, i: (b * nq + i, h)),
        out_shape=jax.ShapeDtypeStruct((batch * seq, FOX_W), BF16),
        compiler_params=_params(("parallel", "parallel", "arbitrary")),
        name="fox_prompt",
    )(z, z, z, c_col, c_row)


HT = 128


def _row_to_col(r, n):
    eye = lax.broadcasted_iota(jnp.int32, (n, n), 0) == lax.broadcasted_iota(jnp.int32, (n, n), 1)
    return jnp.sum(jnp.where(eye, jnp.broadcast_to(r, (n, n)), 0.0), axis=1, keepdims=True)


def _fox_sample_kernel(pt_ref, q_ref, kn_ref, vn_ref, lfn_ref, kc_ref, vc_ref, lfc_ref, o_ref,
                       qbd_ref, cq_ref, carry_ref, m_ref, l_ref, acc_ref, *, dec):
    del pt_ref
    j = pl.program_id(1)
    scale = FOX_DH ** -0.5
    expand = (lax.broadcasted_iota(jnp.int32, (FOX_HEADS, HT), 1) // dec
              == lax.broadcasted_iota(jnp.int32, (FOX_HEADS, HT), 0)).astype(F32)

    def update(s, vv, first):
        m_old = m_ref[...]
        m_blk = jnp.max(s, axis=0, keepdims=True)
        m_new = m_blk if first else jnp.maximum(m_old, m_blk)
        p = jnp.exp(s - m_new)
        pv = _dot_tn(p.astype(BF16), vv)
        if first:
            l_ref[...] = jnp.sum(p, axis=0, keepdims=True)
            acc_ref[...] = pv
        else:
            corr = jnp.exp(m_old - m_new)
            l_ref[...] = l_ref[...] * corr + jnp.sum(p, axis=0, keepdims=True)
            acc_ref[...] = acc_ref[...] * _row_to_col(corr, HT) + pv
        m_ref[...] = m_new

    @pl.when(j == 0)
    def _():
        q = q_ref[...]
        qt = jnp.concatenate([q] * (HT // dec), axis=0)
        rh = lax.broadcasted_iota(jnp.int32, (HT, FOX_W), 0) // dec
        ch = lax.broadcasted_iota(jnp.int32, (HT, FOX_W), 1) // FOX_DH
        qbd = jnp.where(rh == ch, qt, 0.0).astype(BF16)
        qbd_ref[...] = qbd
        tr = lax.broadcasted_iota(jnp.int32, (dec, dec), 0)
        tc = lax.broadcasted_iota(jnp.int32, (dec, dec), 1)
        cq = _dot((tr >= tc).astype(F32), lfn_ref[...], precision=HIGHEST)
        cqe = _dot(cq, expand, precision=HIGHEST)
        srow = lax.broadcasted_iota(jnp.int32, (dec, HT), 0)
        tcol = lax.broadcasted_iota(jnp.int32, (dec, HT), 1) % dec
        cq_row = jnp.sum(jnp.where(srow == tcol, cqe, 0.0), axis=0, keepdims=True)
        cq_ref[...] = cq_row
        carry_ref[...] = jnp.zeros_like(carry_ref)
        s = _dot_nt(kn_ref[...].astype(BF16), qbd) * scale + (cq_row - cqe)
        s = jnp.where(srow <= tcol, s, -jnp.inf)
        update(s, vn_ref[...].astype(BF16), True)

    lfe = _dot(lfc_ref[0], expand, precision=HIGHEST)
    pr = lax.broadcasted_iota(jnp.int32, (PAGE_SIZE, PAGE_SIZE), 0)
    pc = lax.broadcasted_iota(jnp.int32, (PAGE_SIZE, PAGE_SIZE), 1)
    suffix = _dot((pc > pr).astype(F32), lfe, precision=HIGHEST)
    s = (_dot_nt(kc_ref[0].astype(BF16), qbd_ref[...]) * scale
         + (cq_ref[...] + carry_ref[...] + suffix))
    update(s, vc_ref[0].astype(BF16), False)
    carry_ref[...] = carry_ref[...] + jnp.sum(lfe, axis=0, keepdims=True)

    @pl.when(j == pl.num_programs(1) - 1)
    def _():
        inv = 1.0 / _row_to_col(l_ref[...], HT)
        for h in range(FOX_HEADS):
            rs = slice(h * dec, (h + 1) * dec)
            cs = slice(h * FOX_DH, (h + 1) * FOX_DH)
            o_ref[:, cs] = acc_ref[rs, cs] * inv[rs, :]


def _fox_sample(z, lf, cache_k, cache_v, cache_lf, page_table, *, batch, dec):
    assert FOX_HEADS * dec <= HT and dec % 8 == 0
    n_pool = cache_k.shape[0]
    n_pages = page_table.shape[1]
    kc = cache_k.reshape(n_pool, PAGE_SIZE, FOX_W)
    vc = cache_v.reshape(n_pool, PAGE_SIZE, FOX_W)
    pt = page_table.reshape(-1)
    page = lambda b, j, pt_ref: (pt_ref[b * n_pages + (n_pages - 1 - j)], 0, 0)
    grid_spec = pltpu.PrefetchScalarGridSpec(
        num_scalar_prefetch=1,
        grid=(batch, n_pages),
        in_specs=[
            pl.BlockSpec((dec, FOX_W), lambda b, j, pt_ref: (b, OFF_FQ // FOX_W)),
            pl.BlockSpec((dec, FOX_W), lambda b, j, pt_ref: (b, OFF_FK // FOX_W)),
            pl.BlockSpec((dec, FOX_W), lambda b, j, pt_ref: (b, OFF_FV // FOX_W)),
            pl.BlockSpec((dec, FOX_HEADS), lambda b, j, pt_ref: (b, 0)),
            pl.BlockSpec((1, PAGE_SIZE, FOX_W), page),
            pl.BlockSpec((1, PAGE_SIZE, FOX_W), page),
            pl.BlockSpec((1, PAGE_SIZE, FOX_HEADS), page),
        ],
        out_specs=pl.BlockSpec((dec, FOX_W), lambda b, j, pt_ref: (b, 0)),
        scratch_shapes=[
            pltpu.VMEM((HT, FOX_W), BF16),
            pltpu.VMEM((1, HT), F32),
            pltpu.VMEM((1, HT), F32),
            pltpu.VMEM((1, HT), F32),
            pltpu.VMEM((1, HT), F32),
            pltpu.VMEM((HT, FOX_W), F32),
        ],
    )
    return pl.pallas_call(
        functools.partial(_fox_sample_kernel, dec=dec),
        grid_spec=grid_spec,
        out_shape=jax.ShapeDtypeStruct((batch * dec, FOX_W), F32),
        compiler_params=_params(("parallel", "arbitrary")),
        name="fox_sample",
    )(pt, z, z, z, lf, kc, vc, cache_lf)


def _merge_kernel(oa_ref, ob_ref, ga_ref, gb_ref, x_ref, wpa_ref, wpb_ref, wo_ref, y_ref):
    a = _dot(oa_ref[...].astype(BF16), wpa_ref[...])
    b = _dot(ob_ref[...].astype(BF16), wpb_ref[...])
    merged = _sigmoid(ga_ref[...]) * a + _sigmoid(gb_ref[...]) * b
    y_ref[...] = x_ref[...] + _dot(merged.astype(BF16), wo_ref[...])


def _merge(o_gla, o_fox, z, x, w_pa, w_pb, w_o, *, tm):
    t, d = x.shape
    const = lambda shape: pl.BlockSpec(shape, lambda i: (0, 0), pipeline_mode=pl.Buffered(1))
    return pl.pallas_call(
        _merge_kernel,
        grid=(t // tm,),
        in_specs=[
            pl.BlockSpec((tm, GLA_VW), lambda i: (i, 0)),
            pl.BlockSpec((tm, FOX_W), lambda i: (i, 0)),
            pl.BlockSpec((tm, d), lambda i: (i, OFF_GA // d)),
            pl.BlockSpec((tm, d), lambda i: (i, OFF_GA // d + 1)),
            pl.BlockSpec((tm, d), lambda i: (i, 0)),
            const((GLA_VW, d)),
            const((FOX_W, d)),
            const((d, d)),
        ],
        out_specs=pl.BlockSpec((tm, d), lambda i: (i, 0)),
        out_shape=jax.ShapeDtypeStruct((t, d), F32),
        compiler_params=_params(("parallel",)),
        name="merge",
    )(o_gla, o_fox, z, z, x, w_pa, w_pb, w_o)


def _ffn_kernel(x_ref, g_ref, wg_ref, wu_ref, wd_ref, gf_ref, y_ref, h_ref, acc_ref, *, final_norm):
    f = pl.program_id(1)

    @pl.when(f == 0)
    def _():
        h_ref[...] = _rmsnorm(x_ref[...], g_ref[...]).astype(BF16)
        acc_ref[...] = jnp.zeros_like(acc_ref)

    h = h_ref[...]
    act = _silu(_dot(h, wg_ref[...])) * _dot(h, wu_ref[...])
    acc_ref[...] += _dot(act.astype(BF16), wd_ref[...])

    @pl.when(f == pl.num_programs(1) - 1)
    def _():
        y = x_ref[...] + acc_ref[...]
        y_ref[...] = _rmsnorm(y, gf_ref[...]) if final_norm else y


def _ffn(x, g, w_gate, w_up, w_down, g_final, *, tm, tf, final_norm):
    t, d = x.shape
    dff = w_gate.shape[1]
    return pl.pallas_call(
        functools.partial(_ffn_kernel, final_norm=final_norm),
        grid=(t // tm, dff // tf),
        in_specs=[
            pl.BlockSpec((tm, d), lambda i, f: (i, 0)),
            pl.BlockSpec((1, d), lambda i, f: (0, 0)),
            pl.BlockSpec((d, tf), lambda i, f: (0, f)),
            pl.BlockSpec((d, tf), lambda i, f: (0, f)),
            pl.BlockSpec((tf, d), lambda i, f: (f, 0)),
            pl.BlockSpec((1, d), lambda i, f: (0, 0)),
        ],
        out_specs=pl.BlockSpec((tm, d), lambda i, f: (i, 0)),
        out_shape=jax.ShapeDtypeStruct((t, d), F32),
        scratch_shapes=[pltpu.VMEM((tm, d), BF16), pltpu.VMEM((tm, d), F32)],
        compiler_params=_params(("parallel", "arbitrary")),
        name="ffn",
    )(x, g, w_gate, w_up, w_down, g_final)


def _pick(n, prefs):
    for p in prefs:
        if n % p == 0:
            return p
    raise ValueError(f"no tile in {prefs} divides {n}")


def _group_layer(x, lw, g_final, final_norm, *, batch, seq, s0, cache):
    t, d = x.shape
    z, zs, lf = _in_proj(x, lw["g_attn"], lw["w_main"], lw["w_small"], lw["b_fox_f"],
                         tm=_pick(t, (512, 256, 128)), tn=1024)
    prompt = cache is None
    o_gla, s_fin = _gla(z, zs, lw["w_gla_a2"], lw["b_gla_a"], lw["g_gla_norm"], s0,
                        batch=batch, seq=seq, out_dtype=BF16 if prompt else F32)
    if prompt:
        c_col = _fox_cumsum(lf.reshape(batch, seq, FOX_HEADS), blk=_pick(seq, (256, 128)))
        c_row = jnp.swapaxes(c_col, 1, 2).reshape(batch * FOX_HEADS, 1, seq)
        tq = _pick(seq, (256, 128))
        o_fox = _fox_prompt(z, c_col, c_row, batch=batch, seq=seq, tq=tq, tk=tq)
    else:
        o_fox = _fox_sample(z, lf, *cache, batch=batch, dec=seq)
    x1 = _merge(o_gla, o_fox, z, x, lw["w_pa"], lw["w_pb"], lw["w_o"], tm=_pick(t, (256, 128)))
    y = _ffn(x1, lw["g_ffn"], lw["w_gate"], lw["w_up"], lw["w_down"], g_final,
             tm=_pick(t, (512, 256, 128)), tf=_pick(lw["w_gate"].shape[1], (512, 256, 128)),
             final_norm=final_norm)
    new_k = z[:, OFF_FK:OFF_FK + FOX_W].reshape(batch, seq, FOX_HEADS, FOX_DH)
    new_v = z[:, OFF_FV:OFF_FV + FOX_W].reshape(batch, seq, FOX_HEADS, FOX_DH)
    return y, s_fin, new_k, new_v, lf.reshape(batch, seq, FOX_HEADS)


def kernel(x_prompt, x_sample, cache_k, cache_v, cache_lf, state_gla, page_table, g_attn, w_in, w_gla_a2,
           b_gla_a, b_fox_f, g_gla_norm, w_pa, w_pb, w_o, g_ffn, w_gate, w_up, w_down, g_final):
    depth = w_in.shape[0]
    bp, sp, d = x_prompt.shape
    bs, ss, _ = x_sample.shape
    yp = x_prompt.reshape(bp * sp, d)
    ys = x_sample.reshape(bs * ss, d)
    gf = g_final.reshape(1, d)
    a1_lo = 2 * GLA_KW + 2 * GLA_VW
    fq_lo = a1_lo + GLA_RANK
    ff_lo = fq_lo + 3 * FOX_W
    outs_p, outs_s = [], []
    for l in range(depth):
        w = w_in[l]
        pad = jnp.zeros((d, SMALL_W - GLA_RANK - FOX_HEADS), w.dtype)
        lw = {
            "w_main": jnp.concatenate([w[:, :a1_lo], w[:, fq_lo:ff_lo], w[:, ff_lo + FOX_HEADS:]], axis=1).astype(BF16),
            "w_small": jnp.concatenate([w[:, a1_lo:fq_lo], w[:, ff_lo:ff_lo + FOX_HEADS], pad], axis=1).astype(BF16),
            "g_attn": g_attn[l].reshape(1, d),
            "b_fox_f": b_fox_f[l].reshape(1, FOX_HEADS),
            "w_gla_a2": w_gla_a2[l].astype(BF16),
            "b_gla_a": b_gla_a[l].reshape(1, GLA_KW),
            "g_gla_norm": g_gla_norm[l].reshape(1, GLA_VW),
            "w_pa": w_pa[l].astype(BF16),
            "w_pb": w_pb[l].astype(BF16),
            "w_o": w_o[l].astype(BF16),
            "g_ffn": g_ffn[l].reshape(1, d),
            "w_gate": w_gate[l].astype(BF16),
            "w_up": w_up[l].astype(BF16),
            "w_down": w_down[l].astype(BF16),
        }
        last = l == depth - 1
        yp, *rest_p = _group_layer(yp, lw, gf, last, batch=bp, seq=sp, s0=None, cache=None)
        ys, *rest_s = _group_layer(ys, lw, gf, last, batch=bs, seq=ss, s0=state_gla[l],
                                   cache=(cache_k[l], cache_v[l], cache_lf[l], page_table))
        outs_p.append(rest_p)
        outs_s.append(rest_s)
    stack = lambda outs, i: jnp.stack([o[i] for o in outs])
    return (yp.reshape(bp, sp, d), ys.reshape(bs, ss, d), stack(outs_p, 0), stack(outs_s, 0),
            stack(outs_p, 1), stack(outs_p, 2), stack(outs_p, 3),
            stack(outs_s, 1), stack(outs_s, 2), stack(outs_s, 3))
```

```python
import functools
import math

import jax
import jax.numpy as jnp
from jax import lax
from jax.experimental import pallas as pl
from jax.experimental.pallas import tpu as pltpu

F32 = jnp.float32
BF16 = jnp.bfloat16
HIGHEST = lax.Precision.HIGHEST

EPS = 1e-6
GLA_HEADS = 4
GLA_DK = 128
GLA_DV = 256
GLA_KW = GLA_HEADS * GLA_DK
GLA_VW = GLA_HEADS * GLA_DV
GLA_RANK = 16
GLA_TAU = 16.0
GLA_CHUNK = 64
FOX_HEADS = 8
FOX_DH = 128
FOX_W = FOX_HEADS * FOX_DH
PAGE_SIZE = 128
SMALL_W = 128

OFF_GQ = 0
OFF_GK = OFF_GQ + GLA_KW
OFF_GV = OFF_GK + GLA_KW
OFF_GR = OFF_GV + GLA_VW
OFF_FQ = OFF_GR + GLA_VW
OFF_FK = OFF_FQ + FOX_W
OFF_FV = OFF_FK + FOX_W
OFF_GA = OFF_FV + FOX_W

VMEM_LIMIT = 52 * 1024 * 1024


def _params(sem, vmem=VMEM_LIMIT):
    return pltpu.CompilerParams(dimension_semantics=sem, vmem_limit_bytes=vmem)


def _log_sigmoid(x):
    return jnp.minimum(x, 0.0) - jnp.log1p(jnp.exp(-jnp.abs(x)))


def _sigmoid(x):
    return 1.0 / (1.0 + jnp.exp(-x))


def _silu(x):
    return x * _sigmoid(x)


def _rmsnorm(x, g):
    return x * lax.rsqrt(jnp.mean(x * x, axis=-1, keepdims=True) + EPS) * g


def _dot(a, b, **kw):
    return jnp.dot(a, b, preferred_element_type=F32, **kw)


def _dot_nt(a, b):
    return lax.dot_general(a, b, (((1,), (1,)), ((), ())), preferred_element_type=F32)


def _dot_tn(a, b):
    return lax.dot_general(a, b, (((0,), (0,)), ((), ())), preferred_element_type=F32)


def _iota2(shape, axis):
    return lax.broadcasted_iota(jnp.int32, shape, axis)


def _in_proj_kernel(x_ref, g_ref, w_ref, ws_ref, bf_ref, z_ref, zs_ref, lf_ref, k_ref, v_ref, h_ref, *, tm, tn):
    j = pl.program_id(1)

    @pl.when(j == 0)
    def _():
        hb = _rmsnorm(x_ref[...], g_ref[...]).astype(BF16)
        h_ref[...] = hb
        zs = _dot(hb, ws_ref[...])
        zs_ref[...] = zs
        lf_ref[...] = _log_sigmoid(zs[:, GLA_RANK:GLA_RANK + FOX_HEADS] + bf_ref[...])

    zt = _dot(h_ref[...], w_ref[...])
    z_ref[...] = zt

    def store_heads_major(dst_ref):
        for h in range(FOX_HEADS):
            dst_ref[pl.ds(h, tm, stride=FOX_HEADS), :] = zt[:, h * FOX_DH:(h + 1) * FOX_DH]

    @pl.when(j == OFF_FK // tn)
    def _():
        store_heads_major(k_ref)

    @pl.when(j == OFF_FV // tn)
    def _():
        store_heads_major(v_ref)


def _in_proj(x, g, w_main, w_small, b_fox, *, tm):
    t, d = x.shape
    n = w_main.shape[1]
    tn = FOX_W
    return pl.pallas_call(
        functools.partial(_in_proj_kernel, tm=tm, tn=tn),
        grid=(t // tm, n // tn),
        in_specs=[
            pl.BlockSpec((tm, d), lambda i, j: (i, 0)),
            pl.BlockSpec((1, d), lambda i, j: (0, 0)),
            pl.BlockSpec((d, tn), lambda i, j: (0, j)),
            pl.BlockSpec((d, SMALL_W), lambda i, j: (0, 0)),
            pl.BlockSpec((1, FOX_HEADS), lambda i, j: (0, 0)),
        ],
        out_specs=[
            pl.BlockSpec((tm, tn), lambda i, j: (i, j)),
            pl.BlockSpec((tm, SMALL_W), lambda i, j: (i, 0)),
            pl.BlockSpec((tm, FOX_HEADS), lambda i, j: (i, 0)),
            pl.BlockSpec((tm * FOX_HEADS, FOX_DH), lambda i, j: (i, 0)),
            pl.BlockSpec((tm * FOX_HEADS, FOX_DH), lambda i, j: (i, 0)),
        ],
        out_shape=[
            jax.ShapeDtypeStruct((t, n), F32),
            jax.ShapeDtypeStruct((t, SMALL_W), F32),
            jax.ShapeDtypeStruct((t, FOX_HEADS), F32),
            jax.ShapeDtypeStruct((t * FOX_HEADS, FOX_DH), F32),
            jax.ShapeDtypeStruct((t * FOX_HEADS, FOX_DH), F32),
        ],
        scratch_shapes=[pltpu.VMEM((tm, d), BF16)],
        compiler_params=_params(("parallel", "arbitrary")),
        name="in_proj",
    )(x, g, w_main, w_small, b_fox)


def _gla_kernel(*refs, chunk, has_init):
    if has_init:
        (q_ref, k_ref, v_ref, gr_ref, zs_ref, wa2_ref, ba_ref, gn_ref, s0_ref,
         o_ref, sfin_ref, s_scr) = refs
    else:
        (q_ref, k_ref, v_ref, gr_ref, zs_ref, wa2_ref, ba_ref, gn_ref,
         o_ref, sfin_ref, s_scr) = refs
    c = pl.program_id(1)

    @pl.when(c == 0)
    def _():
        if has_init:
            s_scr[...] = s0_ref[0]
        else:
            s_scr[...] = jnp.zeros_like(s_scr)

    ga1 = zs_ref[:, 0:GLA_RANK].astype(BF16)
    log_a = _log_sigmoid(_dot(ga1, wa2_ref[...]) + ba_ref[...]) / GLA_TAU
    causal = _iota2((chunk, chunk), 0) >= _iota2((chunk, chunk), 1)
    b = _dot(causal.astype(F32), log_a, precision=HIGHEST)
    b_last = b[chunk - 1:chunk, :]
    q = q_ref[...] * (GLA_DK ** -0.5)
    k = k_ref[...]
    q_dec = (q * jnp.exp(b)).astype(BF16)
    k_inv = (k * jnp.exp(-b)).astype(BF16)
    k_end = (k * jnp.exp(b_last - b)).astype(BF16)
    d_last = jnp.exp(b_last)

    for h in range(GLA_HEADS):
        ks = slice(h * GLA_DK, (h + 1) * GLA_DK)
        vs = slice(h * GLA_DV, (h + 1) * GLA_DV)
        attn = jnp.where(causal, _dot_nt(q_dec[:, ks], k_inv[:, ks]), 0.0)
        s_old = s_scr[h]
        vh = v_ref[:, vs].astype(BF16)
        o = _dot(q_dec[:, ks], s_old.astype(BF16)) + _dot(attn.astype(BF16), vh)
        d_col = jnp.transpose(jnp.broadcast_to(d_last[:, ks], (GLA_DK, GLA_DK)))
        s_scr[h] = s_old * jnp.concatenate([d_col, d_col], axis=1) + _dot_tn(k_end[:, ks], vh)
        o = o * lax.rsqrt(jnp.mean(o * o, axis=-1, keepdims=True) + EPS)
        o_ref[:, vs] = ((o * gn_ref[:, vs]) * _silu(gr_ref[:, vs])).astype(o_ref.dtype)

    @pl.when(c == pl.num_programs(1) - 1)
    def _():
        sfin_ref[0] = s_scr[...]


def _gla(z, zs, wa2, ba, gn, s0, *, batch, seq, out_dtype):
    chunk = math.gcd(seq, GLA_CHUNK)
    assert chunk % 8 == 0
    n = seq // chunk
    t = batch * seq
    rows = lambda b, c: b * n + c
    kw, vw = GLA_KW, GLA_VW
    in_specs = [
        pl.BlockSpec((chunk, kw), lambda b, c: (rows(b, c), OFF_GQ // kw)),
        pl.BlockSpec((chunk, kw), lambda b, c: (rows(b, c), OFF_GK // kw)),
        pl.BlockSpec((chunk, vw), lambda b, c: (rows(b, c), OFF_GV // vw)),
        pl.BlockSpec((chunk, vw), lambda b, c: (rows(b, c), OFF_GR // vw)),
        pl.BlockSpec((chunk, SMALL_W), lambda b, c: (rows(b, c), 0)),
        pl.BlockSpec((GLA_RANK, kw), lambda b, c: (0, 0)),
        pl.BlockSpec((1, kw), lambda b, c: (0, 0)),
        pl.BlockSpec((1, vw), lambda b, c: (0, 0)),
    ]
    args = [z, z, z, z, zs, wa2, ba, gn]
    state_spec = pl.BlockSpec((1, GLA_HEADS, GLA_DK, GLA_DV), lambda b, c: (b, 0, 0, 0))
    if s0 is not None:
        in_specs.append(state_spec)
        args.append(s0)
    return pl.pallas_call(
        functools.partial(_gla_kernel, chunk=chunk, has_init=s0 is not None),
        grid=(batch, n),
        in_specs=in_specs,
        out_specs=[pl.BlockSpec((chunk, vw), lambda b, c: (rows(b, c), 0)), state_spec],
        out_shape=[
            jax.ShapeDtypeStruct((t, vw), out_dtype),
            jax.ShapeDtypeStruct((batch, GLA_HEADS, GLA_DK, GLA_DV), F32),
        ],
        scratch_shapes=[pltpu.VMEM((GLA_HEADS, GLA_DK, GLA_DV), F32)],
        compiler_params=_params(("parallel", "arbitrary")),
        name="gla",
    )(*args)


def _cumsum_kernel(lf_ref, c_ref, carry_ref, *, blk):
    @pl.when(pl.program_id(1) == 0)
    def _():
        carry_ref[...] = jnp.zeros_like(carry_ref)

    lower = _iota2((blk, blk), 0) >= _iota2((blk, blk), 1)
    c = _dot(lower.astype(F32), lf_ref[0], precision=HIGHEST) + carry_ref[...]
    c_ref[0] = c
    carry_ref[...] = c[blk - 1:blk, :]


def _fox_cumsum(lf, *, blk):
    b, s, h = lf.shape
    return pl.pallas_call(
        functools.partial(_cumsum_kernel, blk=blk),
        grid=(b, s // blk),
        in_specs=[pl.BlockSpec((1, blk, h), lambda i, j: (i, j, 0))],
        out_specs=pl.BlockSpec((1, blk, h), lambda i, j: (i, j, 0)),
        out_shape=jax.ShapeDtypeStruct((b, s, h), F32),
        scratch_shapes=[pltpu.VMEM((1, h), F32)],
        compiler_params=_params(("parallel", "arbitrary")),
        name="fox_cumsum",
    )(lf)


PROMPT_HEAD_GROUP = 4


def _fox_prompt_kernel(q_ref, k_ref, v_ref, cc_ref, cr_ref, o_ref, kb_ref, vb_ref, *, tq):
    hg = pl.program_id(1)
    qi = pl.program_id(2)
    group = PROMPT_HEAD_GROUP

    @pl.when(qi == 0)
    def _():
        kb_ref[...] = k_ref[...].astype(BF16)
        vb_ref[...] = v_ref[...].astype(BF16)

    q = (q_ref[...] * (FOX_DH ** -0.5)).astype(BF16)
    cc = cc_ref[0]
    lane = _iota2((tq, FOX_HEADS), 1)
    cq = [jnp.sum(jnp.where(lane == hg * group + i, cc, 0.0), axis=1, keepdims=True) for i in range(group)]
    causal = _iota2((tq, tq), 0) >= _iota2((tq, tq), 1)

    def block(kb, carry, masked):
        start = pl.multiple_of(kb * tq, tq)
        out = []
        for i in range(group):
            m, l, acc = carry[i]
            hs = slice(i * FOX_DH, (i + 1) * FOX_DH)
            ck = cr_ref[i, :, pl.ds(start, tq)]
            s = _dot_nt(q[:, hs], kb_ref[pl.ds(start, tq), hs]) + (cq[i] - ck)
            if masked:
                s = jnp.where(causal, s, -jnp.inf)
            m_new = jnp.maximum(m, jnp.max(s, axis=1, keepdims=True))
            corr = jnp.exp(m - m_new)
            p = jnp.exp(s - m_new)
            l = l * corr + jnp.sum(p, axis=1, keepdims=True)
            acc = acc * corr + _dot(p.astype(BF16), vb_ref[pl.ds(start, tq), hs])
            out.append((m_new, l, acc))
        return tuple(out)

    init = tuple((jnp.full((tq, 1), -jnp.inf, F32), jnp.zeros((tq, 1), F32), jnp.zeros((tq, FOX_DH), F32))
                 for _ in range(group))
    carry = lax.fori_loop(0, qi, lambda kb, c: block(kb, c, False), init)
    carry = block(qi, carry, True)
    for i in range(group):
        _, l, acc = carry[i]
        o_ref[:, i * FOX_DH:(i + 1) * FOX_DH] = (acc / l).astype(o_ref.dtype)


def _fox_prompt(z, c_col, c_row, *, batch, seq, tq):
    nq = seq // tq
    group = PROMPT_HEAD_GROUP
    gw = group * FOX_DH
    n_groups = FOX_HEADS // group
    return pl.pallas_call(
        functools.partial(_fox_prompt_kernel, tq=tq),
        grid=(batch, n_groups, nq),
        in_specs=[
            pl.BlockSpec((tq, gw), lambda b, g, i: (b * nq + i, OFF_FQ // gw + g)),
            pl.BlockSpec((seq, gw), lambda b, g, i: (b, OFF_FK // gw + g)),
            pl.BlockSpec((seq, gw), lambda b, g, i: (b, OFF_FV // gw + g)),
            pl.BlockSpec((1, tq, FOX_HEADS), lambda b, g, i: (b, i, 0)),
            pl.BlockSpec((group, 1, seq), lambda b, g, i: (b * n_groups + g, 0, 0)),
        ],
        out_specs=pl.BlockSpec((tq, gw), lambda b, g, i: (b * nq + i, g)),
        out_shape=jax.ShapeDtypeStruct((batch * seq, FOX_W), BF16),
        scratch_shapes=[pltpu.VMEM((seq, gw), BF16), pltpu.VMEM((seq, gw), BF16)],
        compiler_params=_params(("parallel", "parallel", "arbitrary")),
        name="fox_prompt",
    )(z, z, z, c_col, c_row)


def _fox_sample_kernel(pt_ref, *refs, dec, n_pages):
    del pt_ref
    q_ref, kn_ref, vn_ref, lfn_ref, lfnt_ref = refs[:5]
    lft_refs = refs[5:5 + n_pages]
    k_refs = refs[5 + n_pages:5 + 2 * n_pages]
    v_refs = refs[5 + 2 * n_pages:5 + 3 * n_pages]
    o_ref, s_ref = refs[5 + 3 * n_pages:]
    npos = n_pages * PAGE_SIZE
    width = npos + PAGE_SIZE
    pair_w = 2 * FOX_DH

    tr, tc = _iota2((dec, dec), 0), _iota2((dec, dec), 1)
    cq = _dot((tr >= tc).astype(F32), lfn_ref[...], precision=HIGHEST)
    cqt = _dot(lfnt_ref[0], (tr <= tc).astype(F32), precision=HIGHEST)

    lft = jnp.concatenate([r[0] for r in lft_refs], axis=0)
    after = (_iota2((PAGE_SIZE, PAGE_SIZE), 0) > _iota2((PAGE_SIZE, PAGE_SIZE), 1)).astype(F32)
    in_page = _dot(lft, after, precision=HIGHEST)
    totals = jnp.sum(lft, axis=1, keepdims=True)
    bias = [None] * n_pages
    carry = jnp.zeros((FOX_HEADS, 1), F32)
    for j in reversed(range(n_pages)):
        rows = slice(j * FOX_HEADS, (j + 1) * FOX_HEADS)
        bias[j] = in_page[rows] + carry
        carry = carry + totals[rows]
    new_bias = jnp.concatenate([-cqt, jnp.zeros((FOX_HEADS, PAGE_SIZE - dec), F32)], axis=1)
    bias_full = jnp.concatenate(bias + [new_bias], axis=1)

    qs = q_ref[...] * (FOX_DH ** -0.5)
    zero = jnp.zeros((dec, FOX_DH), F32)
    q2 = []
    for hp in range(FOX_HEADS // 2):
        qa = qs[:, (2 * hp) * FOX_DH:(2 * hp + 1) * FOX_DH]
        qb = qs[:, (2 * hp + 1) * FOX_DH:(2 * hp + 2) * FOX_DH]
        q2.append(jnp.concatenate([jnp.concatenate([qa, zero], axis=1),
                                   jnp.concatenate([zero, qb], axis=1)], axis=0).astype(BF16))

    def tile(page_refs, g, hp):
        rows = []
        for pg in (2 * g, 2 * g + 1):
            cols = [page_refs[pg][0, pl.ds(2 * hp + i, PAGE_SIZE, stride=FOX_HEADS), :] for i in (0, 1)]
            rows.append(jnp.concatenate(cols, axis=1))
        return jnp.concatenate(rows, axis=0).astype(BF16)

    pad = jnp.zeros((PAGE_SIZE - dec, FOX_W), F32)
    kn = jnp.concatenate([kn_ref[...], pad], axis=0).astype(BF16)
    vn = jnp.concatenate([vn_ref[...], pad], axis=0).astype(BF16)

    for hp in range(FOX_HEADS // 2):
        rows = slice(hp * 2 * dec, (hp + 1) * 2 * dec)
        for g in range(n_pages // 2):
            s_ref[rows, g * 2 * PAGE_SIZE:(g + 1) * 2 * PAGE_SIZE] = _dot_nt(q2[hp], tile(k_refs, g, hp))
        s_ref[rows, npos:width] = _dot_nt(q2[hp], kn[:, hp * pair_w:(hp + 1) * pair_w])

    lane = _iota2((dec, width), 1)
    visible = (lane < npos) | (lane - npos <= _iota2((dec, width), 0))
    for hp in range(FOX_HEADS // 2):
        ps, ls = [], []
        for i in (0, 1):
            h = 2 * hp + i
            s = s_ref[h * dec:(h + 1) * dec, :] + (cq[:, h:h + 1] + bias_full[h:h + 1, :])
            s = jnp.where(visible, s, -jnp.inf)
            p = jnp.exp(s - jnp.max(s, axis=1, keepdims=True))
            ls.append(jnp.sum(p, axis=1, keepdims=True))
            ps.append(p)
        p2 = jnp.concatenate(ps, axis=0).astype(BF16)
        acc = _dot(p2[:, npos:width], vn[:, hp * pair_w:(hp + 1) * pair_w])
        for g in range(n_pages // 2):
            acc = acc + _dot(p2[:, g * 2 * PAGE_SIZE:(g + 1) * 2 * PAGE_SIZE], tile(v_refs, g, hp))
        o_ref[:, (2 * hp) * FOX_DH:(2 * hp + 1) * FOX_DH] = acc[0:dec, 0:FOX_DH] / ls[0]
        o_ref[:, (2 * hp + 1) * FOX_DH:(2 * hp + 2) * FOX_DH] = acc[dec:2 * dec, FOX_DH:pair_w] / ls[1]


def _fox_sample(z, lf, cache_k, cache_v, cache_lf, page_table, *, batch, dec):
    n_pool = cache_k.shape[0]
    n_pages = page_table.shape[1]
    assert n_pages % 2 == 0 and dec % 8 == 0 and dec <= PAGE_SIZE
    kc = cache_k.reshape(n_pool, PAGE_SIZE * FOX_HEADS, FOX_DH)
    vc = cache_v.reshape(n_pool, PAGE_SIZE * FOX_HEADS, FOX_DH)
    lft_cache = jnp.swapaxes(cache_lf, 1, 2)
    lft_new = jnp.swapaxes(lf.reshape(batch, dec, FOX_HEADS), 1, 2)
    pt = page_table.reshape(-1)

    def page(j):
        return lambda b, pt_ref: (pt_ref[b * n_pages + j], 0, 0)

    in_specs = [
        pl.BlockSpec((dec, FOX_W), lambda b, pt_ref: (b, OFF_FQ // FOX_W)),
        pl.BlockSpec((dec, FOX_W), lambda b, pt_ref: (b, OFF_FK // FOX_W)),
        pl.BlockSpec((dec, FOX_W), lambda b, pt_ref: (b, OFF_FV // FOX_W)),
        pl.BlockSpec((dec, FOX_HEADS), lambda b, pt_ref: (b, 0)),
        pl.BlockSpec((1, FOX_HEADS, dec), lambda b, pt_ref: (b, 0, 0)),
    ]
    in_specs += [pl.BlockSpec((1, FOX_HEADS, PAGE_SIZE), page(j)) for j in range(n_pages)]
    for _ in (kc, vc):
        in_specs += [pl.BlockSpec((1, PAGE_SIZE * FOX_HEADS, FOX_DH), page(j)) for j in range(n_pages)]
    grid_spec = pltpu.PrefetchScalarGridSpec(
        num_scalar_prefetch=1,
        grid=(batch,),
        in_specs=in_specs,
        out_specs=pl.BlockSpec((dec, FOX_W), lambda b, pt_ref: (b, 0)),
        scratch_shapes=[pltpu.VMEM((FOX_HEADS * dec, (n_pages + 1) * PAGE_SIZE), F32)],
    )
    return pl.pallas_call(
        functools.partial(_fox_sample_kernel, dec=dec, n_pages=n_pages),
        grid_spec=grid_spec,
        out_shape=jax.ShapeDtypeStruct((batch * dec, FOX_W), F32),
        compiler_params=_params(("parallel",)),
        name="fox_sample",
    )(pt, z, z, z, lf, lft_new, *([lft_cache] * n_pages), *([kc] * n_pages), *([vc] * n_pages))


def _merge_kernel(oa_ref, ob_ref, ga_ref, gb_ref, x_ref, wpa_ref, wpb_ref, wo_ref, y_ref):
    a = _dot(oa_ref[...].astype(BF16), wpa_ref[...])
    b = _dot(ob_ref[...].astype(BF16), wpb_ref[...])
    merged = _sigmoid(ga_ref[...]) * a + _sigmoid(gb_ref[...]) * b
    y_ref[...] = x_ref[...] + _dot(merged.astype(BF16), wo_ref[...])


def _merge(o_gla, o_fox, z, x, w_pa, w_pb, w_o, *, tm):
    t, d = x.shape
    const = lambda shape: pl.BlockSpec(shape, lambda i: (0, 0), pipeline_mode=pl.Buffered(1))
    return pl.pallas_call(
        _merge_kernel,
        grid=(t // tm,),
        in_specs=[
            pl.BlockSpec((tm, GLA_VW), lambda i: (i, 0)),
            pl.BlockSpec((tm, FOX_W), lambda i: (i, 0)),
            pl.BlockSpec((tm, d), lambda i: (i, OFF_GA // d)),
            pl.BlockSpec((tm, d), lambda i: (i, OFF_GA // d + 1)),
            pl.BlockSpec((tm, d), lambda i: (i, 0)),
            const((GLA_VW, d)),
            const((FOX_W, d)),
            const((d, d)),
        ],
        out_specs=pl.BlockSpec((tm, d), lambda i: (i, 0)),
        out_shape=jax.ShapeDtypeStruct((t, d), F32),
        compiler_params=_params(("parallel",)),
        name="merge",
    )(o_gla, o_fox, z, z, x, w_pa, w_pb, w_o)


def _ffn_kernel(x_ref, g_ref, wg_ref, wu_ref, wd_ref, gf_ref, y_ref, h_ref, acc_ref, *, final_norm):
    f = pl.program_id(1)

    @pl.when(f == 0)
    def _():
        h_ref[...] = _rmsnorm(x_ref[...], g_ref[...]).astype(BF16)
        acc_ref[...] = jnp.zeros_like(acc_ref)

    h = h_ref[...]
    act = _silu(_dot(h, wg_ref[...])) * _dot(h, wu_ref[...])
    acc_ref[...] += _dot(act.astype(BF16), wd_ref[...])

    @pl.when(f == pl.num_programs(1) - 1)
    def _():
        y = x_ref[...] + acc_ref[...]
        y_ref[...] = _rmsnorm(y, gf_ref[...]) if final_norm else y


def _ffn(x, g, w_gate, w_up, w_down, g_final, *, tm, tf, final_norm):
    t, d = x.shape
    dff = w_gate.shape[1]
    return pl.pallas_call(
        functools.partial(_ffn_kernel, final_norm=final_norm),
        grid=(t // tm, dff // tf),
        in_specs=[
            pl.BlockSpec((tm, d), lambda i, f: (i, 0)),
            pl.BlockSpec((1, d), lambda i, f: (0, 0)),
            pl.BlockSpec((d, tf), lambda i, f: (0, f)),
            pl.BlockSpec((d, tf), lambda i, f: (0, f)),
            pl.BlockSpec((tf, d), lambda i, f: (f, 0)),
            pl.BlockSpec((1, d), lambda i, f: (0, 0)),
        ],
        out_specs=pl.BlockSpec((tm, d), lambda i, f: (i, 0)),
        out_shape=jax.ShapeDtypeStruct((t, d), F32),
        scratch_shapes=[pltpu.VMEM((tm, d), BF16), pltpu.VMEM((tm, d), F32)],
        compiler_params=_params(("parallel", "arbitrary")),
        name="ffn",
    )(x, g, w_gate, w_up, w_down, g_final)


def _pick(n, prefs):
    for p in prefs:
        if n % p == 0:
            return p
    raise ValueError(f"no tile in {prefs} divides {n}")


def _group_layer(x, lw, g_final, final_norm, *, batch, seq, s0, cache):
    t, d = x.shape
    z, zs, lf, new_k, new_v = _in_proj(x, lw["g_attn"], lw["w_main"], lw["w_small"], lw["b_fox_f"],
                                       tm=_pick(t, (512, 256, 128)))
    prompt = cache is None
    o_gla, s_fin = _gla(z, zs, lw["w_gla_a2"], lw["b_gla_a"], lw["g_gla_norm"], s0,
                        batch=batch, seq=seq, out_dtype=BF16 if prompt else F32)
    if prompt:
        c_col = _fox_cumsum(lf.reshape(batch, seq, FOX_HEADS), blk=_pick(seq, (256, 128)))
        c_row = jnp.swapaxes(c_col, 1, 2).reshape(batch * FOX_HEADS, 1, seq)
        o_fox = _fox_prompt(z, c_col, c_row, batch=batch, seq=seq, tq=_pick(seq, (256, 128)))
    else:
        o_fox = _fox_sample(z, lf, *cache, batch=batch, dec=seq)
    x1 = _merge(o_gla, o_fox, z, x, lw["w_pa"], lw["w_pb"], lw["w_o"], tm=_pick(t, (256, 128)))
    y = _ffn(x1, lw["g_ffn"], lw["w_gate"], lw["w_up"], lw["w_down"], g_final,
             tm=_pick(t, (512, 256, 128)), tf=_pick(lw["w_gate"].shape[1], (512, 256, 128)),
             final_norm=final_norm)
    heads = lambda a: a.reshape(batch, seq, FOX_HEADS, FOX_DH)
    return y, s_fin, heads(new_k), heads(new_v), lf.reshape(batch, seq, FOX_HEADS)


def kernel(x_prompt, x_sample, cache_k, cache_v, cache_lf, state_gla, page_table, g_attn, w_in, w_gla_a2,
           b_gla_a, b_fox_f, g_gla_norm, w_pa, w_pb, w_o, g_ffn, w_gate, w_up, w_down, g_final):
    depth = w_in.shape[0]
    bp, sp, d = x_prompt.shape
    bs, ss, _ = x_sample.shape
    yp = x_prompt.reshape(bp * sp, d)
    ys = x_sample.reshape(bs * ss, d)
    gf = g_final.reshape(1, d)
    a1_lo = 2 * GLA_KW + 2 * GLA_VW
    fq_lo = a1_lo + GLA_RANK
    ff_lo = fq_lo + 3 * FOX_W
    outs_p, outs_s = [], []
    for l in range(depth):
        w = w_in[l]
        pad = jnp.zeros((d, SMALL_W - GLA_RANK - FOX_HEADS), w.dtype)
        lw = {
            "w_main": jnp.concatenate([w[:, :a1_lo], w[:, fq_lo:ff_lo], w[:, ff_lo + FOX_HEADS:]], axis=1).astype(BF16),
            "w_small": jnp.concatenate([w[:, a1_lo:fq_lo], w[:, ff_lo:ff_lo + FOX_HEADS], pad], axis=1).astype(BF16),
            "g_attn": g_attn[l].reshape(1, d),
            "b_fox_f": b_fox_f[l].reshape(1, FOX_HEADS),
            "w_gla_a2": w_gla_a2[l].astype(BF16),
            "b_gla_a": b_gla_a[l].reshape(1, GLA_KW),
            "g_gla_norm": g_gla_norm[l].reshape(1, GLA_VW),
            "w_pa": w_pa[l].astype(BF16),
            "w_pb": w_pb[l].astype(BF16),
            "w_o": w_o[l].astype(BF16),
            "g_ffn": g_ffn[l].reshape(1, d),
            "w_gate": w_gate[l].astype(BF16),
            "w_up": w_up[l].astype(BF16),
            "w_down": w_down[l].astype(BF16),
        }
        last = l == depth - 1
        yp, *rest_p = _group_layer(yp, lw, gf, last, batch=bp, seq=sp, s0=None, cache=None)
        ys, *rest_s = _group_layer(ys, lw, gf, last, batch=bs, seq=ss, s0=state_gla[l],
                                   cache=(cache_k[l], cache_v[l], cache_lf[l], page_table))
        outs_p.append(rest_p)
        outs_s.append(rest_s)
    stack = lambda outs, i: jnp.stack([o[i] for o in outs])
    return (yp.reshape(bp, sp, d), ys.reshape(bs, ss, d), stack(outs_p, 0), stack(outs_s, 0),
            stack(outs_p, 1), stack(outs_p, 2), stack(outs_p, 3),
            stack(outs_s, 1), stack(outs_s, 2), stack(outs_s, 3))
```
